```python
import math, functools
import jax, jax.numpy as jnp
from jax import lax
import numpy as np

D_MODEL = 2048
BATCH = 4
SEQ = 2048
DEPTH = 1
DEC_BATCH = 32
DEC_SEQ = 4
PAST_LEN = 8192
PAGE_SIZE = 128

MIX_W = D_MODEL
ATT_W = MIX_W // 2
POOL_W = MIX_W - ATT_W
ATT_HEAD_DIM = 128
ATT_HEADS = ATT_W // ATT_HEAD_DIM
POOL_WINDOWS = (2, 4, 8, 16)
POOL_GROUP_W = POOL_W // len(POOL_WINDOWS)
POOL_HIST = max(POOL_WINDOWS) - 1
IN_W = 3 * ATT_W + ATT_HEADS + POOL_W
Q_BLOCK = 128
PEER_HEADS = 8
PEER_NKEYS = 128
PEER_EXPERTS = PEER_NKEYS * PEER_NKEYS
PEER_TOPK = 16
PEER_QDIM = 256
PEER_HALF = PEER_QDIM // 2
PEER_BLOCK = 128
NORM_EPS = 1e-6
NEG_INF = -1e30

kernel_name = 'fox_pool_peer_hybrid_step'


def _rmsnorm(x, g):
    xf = x.astype(jnp.float32)
    y = xf * lax.rsqrt(jnp.mean(xf * xf, axis=-1, keepdims=True) + NORM_EPS)
    return y.astype(x.dtype) * g


def _fox_attend(q, f_q, q_pos, k, v, f_k, k_pos):
    s = jnp.einsum('bqhd,bkhd->bhqk', q, k).astype(jnp.float32) * (ATT_HEAD_DIM ** -0.5)
    s = s + jnp.swapaxes(f_q, 1, 2)[..., :, None] - jnp.swapaxes(f_k, 1, 2)[..., None, :]
    s = jnp.where(k_pos[None, :] <= q_pos[:, None], s, NEG_INF)
    pr = jax.nn.softmax(s, axis=-1)
    return jnp.einsum('bhqk,bkhd->bqhd', pr.astype(v.dtype), v)


def _fox_prompt(q, k, v, logf):
    b, t, h, dh = q.shape
    f = jnp.cumsum(logf, axis=1)
    nb = t // Q_BLOCK
    qb = jnp.swapaxes(q.reshape(b, nb, Q_BLOCK, h, dh), 0, 1)
    fb = jnp.swapaxes(f.reshape(b, nb, Q_BLOCK, h), 0, 1)
    pb = jnp.arange(t).reshape(nb, Q_BLOCK)
    k_pos = jnp.arange(t)
    out = lax.map(lambda a: _fox_attend(a[0], a[1], a[2], k, v, f, k_pos), (qb, fb, pb))
    return jnp.swapaxes(out, 0, 1).reshape(b, t, h * dh)


def _pool_mix(p_ext, pos0, w_pool, pool_scale):
    b, le, c = p_ext.shape
    t = le - POOL_HIST
    pf = p_ext.astype(jnp.float32)
    cs = jnp.concatenate([jnp.zeros((b, 1, c), jnp.float32), jnp.cumsum(pf, axis=1)], axis=1)
    pos = pos0 + jnp.arange(t)
    outs = []
    for g, w in enumerate(POOL_WINDOWS):
        sl = slice(g * POOL_GROUP_W, (g + 1) * POOL_GROUP_W)
        ssum = cs[:, POOL_HIST + 1:POOL_HIST + 1 + t, sl] - cs[:, POOL_HIST + 1 - w:POOL_HIST + 1 - w + t, sl]
        cnt = jnp.minimum(pos + 1, w).astype(jnp.float32)
        outs.append(ssum / cnt[None, :, None])
    pooled = jnp.concatenate(outs, axis=-1) - pf[:, POOL_HIST:]
    pooled = pooled.reshape(b, t, len(POOL_WINDOWS), POOL_GROUP_W)
    y = jnp.einsum('btgc,gcd->btgd', pooled.astype(w_pool.dtype), w_pool).reshape(b, t, c)
    return y * pool_scale


def _prompt_mixer(q, k, v, logf, p, w_pool, pool_scale):
    b = q.shape[0]
    att = _fox_prompt(q, k, v, logf)
    p_ext = jnp.concatenate([jnp.zeros((b, POOL_HIST, POOL_W), p.dtype), p], axis=1)
    pool = _pool_mix(p_ext, 0, w_pool, pool_scale)
    return att, pool, (k, v, logf, p_ext[:, -POOL_HIST:])


def _sample_mixer(q, k, v, logf, p, cache_k, cache_v, cache_logf, state_pool, page_table, w_pool, pool_scale):
    b, t, h, dh = q.shape
    past = page_table.shape[1] * PAGE_SIZE
    k_past = cache_k[page_table].reshape(b, past, h, dh).astype(k.dtype)
    v_past = cache_v[page_table].reshape(b, past, h, dh).astype(v.dtype)
    lf_past = cache_logf[page_table].reshape(b, past, h).astype(jnp.float32)
    k_all = jnp.concatenate([k_past, k], axis=1)
    v_all = jnp.concatenate([v_past, v], axis=1)
    f = jnp.cumsum(jnp.concatenate([lf_past, logf], axis=1), axis=1)
    q_pos = past + jnp.arange(t)
    k_pos = jnp.arange(past + t)
    att = _fox_attend(q, f[:, past:], q_pos, k_all, v_all, f, k_pos).reshape(b, t, h * dh)
    p_ext = jnp.concatenate([state_pool.astype(p.dtype), p], axis=1)
    pool = _pool_mix(p_ext, past, w_pool, pool_scale)
    return att, pool, (k, v, logf, p_ext[:, -POOL_HIST:])


def _peer(h, w_q, sub_keys, u_tab, v_tab):
    n, d = h.shape
    nb = -(-n // PEER_BLOCK)
    hp = jnp.pad(h, ((0, nb * PEER_BLOCK - n), (0, 0))).reshape(nb, PEER_BLOCK, d)

    def block(hb):
        q = (hb @ w_q).reshape(PEER_BLOCK, PEER_HEADS, 2, PEER_HALF)
        s = jnp.einsum('thcd,hckd->thck', q, sub_keys).astype(jnp.float32)
        s1, i1 = lax.top_k(s[:, :, 0], PEER_TOPK)
        s2, i2 = lax.top_k(s[:, :, 1], PEER_TOPK)
        n_cand = PEER_TOPK * PEER_TOPK
        cand = (s1[..., :, None] + s2[..., None, :]).reshape(PEER_BLOCK, PEER_HEADS, n_cand)
        cidx = (i1[..., :, None] * PEER_NKEYS + i2[..., None, :]).reshape(PEER_BLOCK, PEER_HEADS, n_cand)
        top, sel = lax.top_k(cand, PEER_TOPK)
        eidx = jnp.take_along_axis(cidx, sel, axis=-1)
        gate = jax.nn.softmax(top, axis=-1)
        act = jax.nn.gelu(jnp.einsum('td,thkd->thk', hb, u_tab[eidx]).astype(jnp.float32), approximate=False)
        wts = (gate * act).astype(hb.dtype)
        return jnp.einsum('thk,thkd->td', wts, v_tab[eidx])

    return lax.map(block, hp).reshape(nb * PEER_BLOCK, d)[:n]


def _layer(x, c, mixer, w_ada, b_ada, g_pre_mix, g_post_mix, g_pre_ffn, g_post_ffn,
           w_in, b_f, w_out, w_peer_q, peer_subkeys, peer_u, peer_v):
    b, t, d = x.shape
    ada = (jax.nn.silu(c) @ w_ada + b_ada)[:, None, :]
    sh1, sc1, gt1, sh2, sc2, gt2 = jnp.split(ada, 6, axis=-1)
    h = _rmsnorm(x, g_pre_mix) * (1 + sc1) + sh1
    u = h @ w_in
    q = u[..., :ATT_W].reshape(b, t, ATT_HEADS, ATT_HEAD_DIM)
    k = u[..., ATT_W:2 * ATT_W].reshape(b, t, ATT_HEADS, ATT_HEAD_DIM)
    v = u[..., 2 * ATT_W:3 * ATT_W].reshape(b, t, ATT_HEADS, ATT_HEAD_DIM)
    logf = jax.nn.log_sigmoid((u[..., 3 * ATT_W:3 * ATT_W + ATT_HEADS] + b_f).astype(jnp.float32))
    p = u[..., 3 * ATT_W + ATT_HEADS:]
    att, pool, state = mixer(q, k, v, logf, p)
    m = jnp.concatenate([att.astype(x.dtype), pool.astype(x.dtype)], axis=-1) @ w_out
    x = x + gt1 * _rmsnorm(m, g_post_mix)
    h2 = _rmsnorm(x, g_pre_ffn) * (1 + sc2) + sh2
    f = _peer(h2.reshape(b * t, d), w_peer_q, peer_subkeys, peer_u, peer_v).reshape(b, t, d)
    x = x + gt2 * _rmsnorm(f, g_post_ffn)
    return x, state


def setup_inputs(seed: int = 0) -> dict:
    key = jax.random.key(seed)
    ks = jax.random.split(key, 28)
    f32 = jnp.float32
    d = D_MODEL
    n_pages = PAST_LEN // PAGE_SIZE
    n_used = DEC_BATCH * n_pages
    n_pool = n_used + n_used // 4

    def nrm(k, shape, scale=1.0):
        return scale * jax.random.normal(k, shape, f32)

    x_prompt = nrm(ks[0], (BATCH, SEQ, d))
    x_sample = nrm(ks[1], (DEC_BATCH, DEC_SEQ, d))
    c_prompt = nrm(ks[2], (BATCH, d))
    c_sample = nrm(ks[3], (DEC_BATCH, d))
    cache_k = nrm(ks[4], (DEPTH, n_pool, PAGE_SIZE, ATT_HEADS, ATT_HEAD_DIM))
    cache_v = nrm(ks[5], (DEPTH, n_pool, PAGE_SIZE, ATT_HEADS, ATT_HEAD_DIM))
    cache_logf = jax.nn.log_sigmoid(4.0 + nrm(ks[6], (DEPTH, n_pool, PAGE_SIZE, ATT_HEADS)))
    state_pool = nrm(ks[7], (DEPTH, DEC_BATCH, POOL_HIST, POOL_W))
    page_table = jax.random.permutation(ks[8], n_pool)[:n_used].reshape(DEC_BATCH, n_pages).astype(jnp.int32)
    w_ada = nrm(ks[9], (DEPTH, d, 6 * d), 0.5 * d ** -0.5)
    b_ada = nrm(ks[10], (DEPTH, 6 * d), 0.02)
    g_pre_mix = 1.0 + nrm(ks[11], (DEPTH, d), 0.05)
    g_post_mix = 1.0 + nrm(ks[12], (DEPTH, d), 0.05)
    g_pre_ffn = 1.0 + nrm(ks[13], (DEPTH, d), 0.05)
    g_post_ffn = 1.0 + nrm(ks[14], (DEPTH, d), 0.05)
    w_qkv = nrm(ks[15], (DEPTH, d, 3 * ATT_W), d ** -0.5)
    w_fg = nrm(ks[16], (DEPTH, d, ATT_HEADS), 0.5 * d ** -0.5)
    w_pin = nrm(ks[17], (DEPTH, d, POOL_W), d ** -0.5)
    w_in = jnp.concatenate([w_qkv, w_fg, w_pin], axis=-1)
    b_f = jnp.linspace(1.0, 6.0, ATT_HEADS, dtype=f32) + nrm(ks[18], (DEPTH, ATT_HEADS), 0.1)
    w_pool = nrm(ks[19], (DEPTH, len(POOL_WINDOWS), POOL_GROUP_W, POOL_GROUP_W), POOL_GROUP_W ** -0.5)
    pool_scale = 1.0 + nrm(ks[20], (DEPTH, POOL_W), 0.1)
    w_out = nrm(ks[21], (DEPTH, MIX_W, d), MIX_W ** -0.5)
    w_peer_q = nrm(ks[22], (DEPTH, d, PEER_HEADS * PEER_QDIM), d ** -0.5)
    peer_subkeys = nrm(ks[23], (DEPTH, PEER_HEADS, 2, PEER_NKEYS, PEER_HALF), PEER_HALF ** -0.5)
    peer_u = nrm(ks[24], (DEPTH, PEER_EXPERTS, d), d ** -0.5)
    peer_v = nrm(ks[25], (DEPTH, PEER_EXPERTS, d), 1.0)
    return {'x_prompt': x_prompt, 'x_sample': x_sample, 'c_prompt': c_prompt, 'c_sample': c_sample,
            'cache_k': cache_k, 'cache_v': cache_v, 'cache_logf': cache_logf, 'state_pool': state_pool,
            'page_table': page_table, 'w_ada': w_ada, 'b_ada': b_ada,
            'g_pre_mix': g_pre_mix, 'g_post_mix': g_post_mix, 'g_pre_ffn': g_pre_ffn, 'g_post_ffn': g_post_ffn,
            'w_in': w_in, 'b_f': b_f, 'w_pool': w_pool, 'pool_scale': pool_scale, 'w_out': w_out,
            'w_peer_q': w_peer_q, 'peer_subkeys': peer_subkeys, 'peer_u': peer_u, 'peer_v': peer_v}


def reference(x_prompt, x_sample, c_prompt, c_sample, cache_k, cache_v, cache_logf, state_pool,
              page_table, w_ada, b_ada, g_pre_mix, g_post_mix, g_pre_ffn, g_post_ffn,
              w_in, b_f, w_pool, pool_scale, w_out, w_peer_q, peer_subkeys, peer_u, peer_v):
    yp = x_prompt
    ys = x_sample
    kp, vp, lp, pp = [], [], [], []
    ks_, vs_, ls_, ps_ = [], [], [], []
    for l in range(DEPTH):
        shared = (w_ada[l], b_ada[l], g_pre_mix[l], g_post_mix[l], g_pre_ffn[l], g_post_ffn[l],
                  w_in[l], b_f[l], w_out[l], w_peer_q[l], peer_subkeys[l], peer_u[l], peer_v[l])
        pmix = functools.partial(_prompt_mixer, w_pool=w_pool[l], pool_scale=pool_scale[l])
        yp, st_p = _layer(yp, c_prompt, pmix, *shared)
        smix = functools.partial(_sample_mixer, cache_k=cache_k[l], cache_v=cache_v[l],
                                 cache_logf=cache_logf[l], state_pool=state_pool[l],
                                 page_table=page_table, w_pool=w_pool[l], pool_scale=pool_scale[l])
        ys, st_s = _layer(ys, c_sample, smix, *shared)
        kp.append(st_p[0]); vp.append(st_p[1]); lp.append(st_p[2]); pp.append(st_p[3])
        ks_.append(st_s[0]); vs_.append(st_s[1]); ls_.append(st_s[2]); ps_.append(st_s[3])
    k_prompt = jnp.stack(kp)
    v_prompt = jnp.stack(vp)
    logf_prompt = jnp.stack(lp)
    pool_prompt = jnp.stack(pp)
    k_sample = jnp.stack(ks_)
    v_sample = jnp.stack(vs_)
    logf_sample = jnp.stack(ls_)
    pool_sample = jnp.stack(ps_)
    return (yp, ys, k_prompt, v_prompt, logf_prompt, pool_prompt, k_sample, v_sample, logf_sample, pool_sample)
```

```python
import functools

import jax
import jax.numpy as jnp
from jax import lax
from jax.experimental import pallas as pl
from jax.experimental.pallas import tpu as pltpu

F32 = jnp.float32
BF16 = jnp.bfloat16

NORM_EPS = 1e-6
NEG_INF = -1e30
PAGE_SIZE = 128
POOL_WINDOWS = (2, 4, 8, 16)
POOL_HIST = max(POOL_WINDOWS) - 1
PEER_TOPK = 16
PEER_NKEYS = 128
V7X_VMEM_LIMIT = 56 * 1024 * 1024

_NT = (((1,), (1,)), ((), ()))
_TN = (((0,), (0,)), ((), ()))


def _params(sem):
    return pltpu.CompilerParams(dimension_semantics=sem, vmem_limit_bytes=V7X_VMEM_LIMIT)


def _rms(x):
    return x * lax.rsqrt(jnp.mean(x * x, axis=-1, keepdims=True) + NORM_EPS)


def _ada_kernel(c_ref, w_ref, b_ref, o_ref):
    c = c_ref[...]
    s = (c * jax.nn.sigmoid(c)).astype(BF16)
    o_ref[...] = jnp.dot(s, w_ref[...].astype(BF16), preferred_element_type=F32) + b_ref[...]


def _ada(c_all, w_ada, b_ada, tn=1024):
    r, d = c_all.shape
    n = w_ada.shape[1]
    return pl.pallas_call(
        _ada_kernel,
        grid=(n // tn,),
        in_specs=[pl.BlockSpec((r, d), lambda j: (0, 0)),
                  pl.BlockSpec((d, tn), lambda j: (0, j)),
                  pl.BlockSpec((1, tn), lambda j: (0, j))],
        out_specs=pl.BlockSpec((r, tn), lambda j: (0, j)),
        out_shape=jax.ShapeDtypeStruct((r, n), F32),
        compiler_params=_params(("arbitrary",)),
        name="ada",
    )(c_all, w_ada, b_ada.reshape(1, n))


def _inproj_kernel(x_ref, sc_ref, sh_ref, g_ref, w_ref, wfg_ref, bf_ref,
                   q_ref, k_ref, v_ref, p_ref, lf_ref, h_scr):
    j = pl.program_id(1)

    @pl.when(j == 0)
    def _():
        h = (_rms(x_ref[...]) * g_ref[...]) * (1.0 + sc_ref[0]) + sh_ref[0]
        hb = h.astype(BF16)
        h_scr[...] = hb
        z = lax.dot_general(wfg_ref[...], hb, _NT, preferred_element_type=F32)[:8] + bf_ref[...]
        lf_ref[...] = jnp.minimum(z, 0.0) - jnp.log1p(jnp.exp(-jnp.abs(z)))

    u = jnp.dot(h_scr[...], w_ref[...], preferred_element_type=F32)

    @pl.when(j == 0)
    def _():
        q_ref[...] = u.astype(BF16)

    @pl.when(j == 1)
    def _():
        k_ref[...] = u

    @pl.when(j == 2)
    def _():
        v_ref[...] = u

    @pl.when(j == 3)
    def _():
        p_ref[...] = u


def _inproj(x, sc, sh, g, w_main, w_fg_t, b_f, tm, rows_per_mod):
    m, d = x.shape
    wcol = w_main.shape[1] // 4
    nh = b_f.shape[0]
    mod_spec = pl.BlockSpec((1,) + sc.shape[1:], lambda i, j: ((i * tm) // rows_per_mod, 0, 0))
    row_spec = lambda: pl.BlockSpec((tm, wcol), lambda i, j: (i, 0))
    return pl.pallas_call(
        _inproj_kernel,
        grid=(m // tm, 4),
        in_specs=[pl.BlockSpec((tm, d), lambda i, j: (i, 0)),
                  mod_spec, mod_spec,
                  pl.BlockSpec((1, d), lambda i, j: (0, 0)),
                  pl.BlockSpec((d, wcol), lambda i, j: (0, j)),
                  pl.BlockSpec(w_fg_t.shape, lambda i, j: (0, 0)),
                  pl.BlockSpec((nh, 1), lambda i, j: (0, 0))],
        out_specs=[row_spec(), row_spec(), row_spec(), row_spec(),
                   pl.BlockSpec((nh, tm), lambda i, j: (0, i))],
        out_shape=[jax.ShapeDtypeStruct((m, wcol), BF16),
                   jax.ShapeDtypeStruct((m, wcol), F32),
                   jax.ShapeDtypeStruct((m, wcol), F32),
                   jax.ShapeDtypeStruct((m, wcol), F32),
                   jax.ShapeDtypeStruct((nh, m), F32)],
        scratch_shapes=[pltpu.VMEM((tm, d), BF16)],
        compiler_params=_params(("arbitrary", "arbitrary")),
        name="inproj",
    )(x, sc, sh, g.reshape(1, d), w_main, w_fg_t, b_f.reshape(nh, 1))


def _lane_scan(x):
    lane = lax.broadcasted_iota(jnp.int32, x.shape, 1)
    s = 1
    while s < x.shape[1]:
        x = x + jnp.where(lane >= s, pltpu.roll(x, s, 1), 0.0)
        s *= 2
    return x


def _cumsum_kernel(lf_ref, f_ref):
    t = lf_ref.shape[1]
    carry = jnp.zeros((lf_ref.shape[0], 1), F32)
    for c in range(t // 128):
        blk = _lane_scan(lf_ref[:, c * 128:(c + 1) * 128]) + carry
        f_ref[:, c * 128:(c + 1) * 128] = blk
        carry = blk[:, 127:128]


def _cumsum(lf_t, nb):
    nh, m = lf_t.shape
    t = m // nb
    return pl.pallas_call(
        _cumsum_kernel,
        grid=(nb,),
        in_specs=[pl.BlockSpec((nh, t), lambda b: (0, b))],
        out_specs=pl.BlockSpec((nh, t), lambda b: (0, b)),
        out_shape=jax.ShapeDtypeStruct((nh, m), F32),
        compiler_params=_params(("arbitrary",)),
        name="cumsum",
    )(lf_t)


def _attn_kernel(q_ref, k_ref, v_ref, f_ref, o_ref, m_scr, l_scr, acc_scr, *, scale):
    qi = pl.program_id(2)
    ki = pl.program_id(3)
    tq = q_ref.shape[1]
    tk = k_ref.shape[1]

    @pl.when(ki == 0)
    def _():
        m_scr[...] = jnp.full(m_scr.shape, NEG_INF, F32)
        l_scr[...] = jnp.zeros(l_scr.shape, F32)
        acc_scr[...] = jnp.zeros(acc_scr.shape, F32)

    @pl.when(ki <= qi)
    def _():
        s = lax.dot_general(q_ref[0], k_ref[0].astype(BF16), _NT, preferred_element_type=F32) * scale
        s = s - f_ref[0]
        row = qi * tq + lax.broadcasted_iota(jnp.int32, (tq, tk), 0)
        col = ki * tk + lax.broadcasted_iota(jnp.int32, (tq, tk), 1)
        s = jnp.where(col <= row, s, NEG_INF)
        m_prev = m_scr[...]
        m_new = jnp.maximum(m_prev, jnp.max(s, axis=-1, keepdims=True))
        alpha = jnp.exp(m_prev - m_new)
        p = jnp.exp(s - m_new)
        l_scr[...] = alpha * l_scr[...] + jnp.sum(p, axis=-1, keepdims=True)
        acc_scr[...] = alpha * acc_scr[...] + jnp.dot(p.astype(BF16), v_ref[0].astype(BF16),
                                                      preferred_element_type=F32)
        m_scr[...] = m_new

    @pl.when(ki == pl.num_programs(3) - 1)
    def _():
        o_ref[0] = (acc_scr[...] / l_scr[...]).astype(o_ref.dtype)


def _prompt_attention(q, k, v, f_rows, nb, nh, dh, tq, tk):
    t = q.shape[1]
    nq, nk = t // tq, t // tk
    kv_spec = pl.BlockSpec((1, tk, dh), lambda b, h, qi, ki: (b, jnp.minimum(ki, qi), h))
    return pl.pallas_call(
        functools.partial(_attn_kernel, scale=dh ** -0.5),
        grid=(nb, nh, nq, nk),
        in_specs=[pl.BlockSpec((1, tq, dh), lambda b, h, qi, ki: (b, qi, h)),
                  kv_spec, kv_spec,
                  pl.BlockSpec((1, 1, tk), lambda b, h, qi, ki: (h, 0, b * nk + jnp.minimum(ki, qi)))],
        out_specs=pl.BlockSpec((1, tq, dh), lambda b, h, qi, ki: (b, qi, h)),
        out_shape=jax.ShapeDtypeStruct(q.shape, BF16),
        scratch_shapes=[pltpu.VMEM((tq, 1), F32), pltpu.VMEM((tq, 1), F32), pltpu.VMEM((tq, dh), F32)],
        compiler_params=_params(("arbitrary",) * 4),
        name="attn",
    )(q, k, v, f_rows)


def _pool_prompt_kernel(p_ref, w_ref, sc_ref, o_ref, pe_scr, *, chunk):
    t = p_ref.shape[1]
    hist = pe_scr.shape[0] - t
    pe_scr[0:hist, :] = jnp.zeros((hist, pe_scr.shape[1]), F32)
    pe_scr[hist:, :] = p_ref[0]
    for g, w in enumerate(POOL_WINDOWS):
        @pl.when(pl.program_id(1) == g)
        def _(w=w):
            wb = w_ref[0].astype(BF16)
            for c in range(t // chunk):
                r0 = hist + c * chunk
                ssum = pe_scr[r0:r0 + chunk, :]
                for dlt in range(1, w):
                    ssum = ssum + pe_scr[r0 - dlt:r0 - dlt + chunk, :]
                pos = c * chunk + lax.broadcasted_iota(jnp.int32, (chunk, 1), 0)
                cnt = jnp.minimum(pos + 1, w).astype(F32)
                pooled = ssum / cnt - pe_scr[r0:r0 + chunk, :]
                y = jnp.dot(pooled.astype(BF16), wb, preferred_element_type=F32) * sc_ref[0]
                o_ref[0, c * chunk:(c + 1) * chunk, :] = y.astype(o_ref.dtype)


def _pool_prompt(p, w_pool, pool_scale, chunk=512):
    nb, t, c = p.shape
    ng, gw, _ = w_pool.shape
    chunk = min(chunk, t)
    return pl.pallas_call(
        functools.partial(_pool_prompt_kernel, chunk=chunk),
        grid=(nb, ng),
        in_specs=[pl.BlockSpec((1, t, gw), lambda b, g: (b, 0, g)),
                  pl.BlockSpec((1, gw, gw), lambda b, g: (g, 0, 0)),
                  pl.BlockSpec((1, 1, gw), lambda b, g: (g, 0, 0))],
        out_specs=pl.BlockSpec((1, t, gw), lambda b, g: (b, 0, g)),
        out_shape=jax.ShapeDtypeStruct(p.shape, BF16),
        scratch_shapes=[pltpu.VMEM((POOL_HIST + 1 + t, gw), F32)],
        compiler_params=_params(("arbitrary", "arbitrary")),
        name="pool_prompt",
    )(p, w_pool, pool_scale.reshape(ng, 1, gw))


def _pool_sample_kernel(pe_ref, w_ref, sc_ref, o_ref, *, pos0):
    nt = o_ref.shape[0]
    gw = w_ref.shape[1]
    for g, w in enumerate(POOL_WINDOWS):
        wb = w_ref[g].astype(BF16)
        cols = slice(g * gw, (g + 1) * gw)
        for t in range(nt):
            ssum = pe_ref[POOL_HIST + t, :, cols]
            for dlt in range(1, w):
                ssum = ssum + pe_ref[POOL_HIST + t - dlt, :, cols]
            pooled = ssum / float(min(pos0 + t + 1, w)) - pe_ref[POOL_HIST + t, :, cols]
            y = jnp.dot(pooled.astype(BF16), wb, preferred_element_type=F32) * sc_ref[:, cols]
            o_ref[t, :, cols] = y.astype(o_ref.dtype)


def _pool_sample(pe_t, w_pool, pool_scale, pos0):
    nt = pe_t.shape[0] - POOL_HIST
    _, nb, c = pe_t.shape
    return pl.pallas_call(
        functools.partial(_pool_sample_kernel, pos0=pos0),
        out_shape=jax.ShapeDtypeStruct((nt, nb, c), BF16),
        compiler_params=pltpu.CompilerParams(vmem_limit_bytes=V7X_VMEM_LIMIT),
        name="pool_sample",
    )(pe_t, w_pool, pool_scale.reshape(1, c))


def _split3(x):
    hi = x.astype(BF16)
    r1 = x - hi.astype(F32)
    mid = r1.astype(BF16)
    lo = (r1 - mid.astype(F32)).astype(BF16)
    return hi, mid, lo


def _lfscan_kernel(x_ref, t_ref, o_ref):
    hi, mid, lo = _split3(x_ref[...])
    tm = t_ref[...]
    o_ref[...] = (jnp.dot(hi, tm, preferred_element_type=F32) + jnp.dot(mid, tm, preferred_element_type=F32)
                  + jnp.dot(lo, tm, preferred_element_type=F32))


def _lfscan(lf_rows, nh, rows):
    n, l = lf_rows.shape
    a = jnp.arange(l)
    same_head = (a[:, None] % nh) == (a[None, :] % nh)
    t_scan = same_head & (a[:, None] // nh <= a[None, :] // nh)
    t_cat = jnp.concatenate([t_scan, same_head], axis=1).astype(BF16)
    rows = min(rows, n)
    return pl.pallas_call(
        _lfscan_kernel,
        grid=(n // rows,),
        in_specs=[pl.BlockSpec((rows, l), lambda i: (i, 0)),
                  pl.BlockSpec((l, 2 * l), lambda i: (0, 0))],
        out_specs=pl.BlockSpec((rows, 2 * l), lambda i: (i, 0)),
        out_shape=jax.ShapeDtypeStruct((n, 2 * l), F32),
        compiler_params=_params(("arbitrary",)),
        name="lfscan",
    )(lf_rows, t_cat)


def _sattn_kernel(pt_ref, q_ref, kn_ref, vn_ref, fn_ref, *rest, pp, nh, scale):
    k_refs = rest[0:pp]
    v_refs = rest[pp:2 * pp]
    f_refs = rest[2 * pp:3 * pp]
    o_ref, m_scr, l_scr, acc_scr, off_scr = rest[3 * pp:]
    del pt_ref
    step = pl.program_id(1)
    nr = q_ref.shape[1]
    lanes = k_refs[0].shape[1]

    @pl.when(step == 0)
    def _():
        m_scr[...] = jnp.full(m_scr.shape, NEG_INF, F32)
        l_scr[...] = jnp.zeros(l_scr.shape, F32)
        acc_scr[...] = jnp.zeros(acc_scr.shape, F32)
        off_scr[...] = jnp.zeros(off_scr.shape, F32)

    q = q_ref[0]

    def attend(k2d, v2d, fk, valid):
        s = lax.dot_general(q, k2d.astype(BF16), _NT, preferred_element_type=F32) * scale - fk
        s = jnp.where(valid, s, NEG_INF)
        m_prev = m_scr[...]
        m_new = jnp.maximum(m_prev, jnp.max(s, axis=-1, keepdims=True))
        alpha = jnp.exp(m_prev - m_new)
        p = jnp.exp(s - m_new)
        l_scr[...] = alpha * l_scr[...] + jnp.sum(p, axis=-1, keepdims=True)
        acc_scr[...] = alpha * acc_scr[...] + jnp.dot(p.astype(BF16), v2d.astype(BF16),
                                                      preferred_element_type=F32)
        m_scr[...] = m_new

    row = lax.broadcasted_iota(jnp.int32, (nr, lanes), 0)
    lane = lax.broadcasted_iota(jnp.int32, (nr, lanes), 1)
    same_head = (row % nh) == (lane % nh)
    for i in range(pp):
        pf = f_refs[i][0]
        off = off_scr[...]
        attend(k_refs[i][0], v_refs[i][0], off + pf[:, :lanes], same_head)
        off_scr[...] = off + pf[:, lanes:]

    @pl.when(step == pl.num_programs(1) - 1)
    def _():
        nn = kn_ref.shape[1]
        rown = lax.broadcasted_iota(jnp.int32, (nr, nn), 0)
        lanen = lax.broadcasted_iota(jnp.int32, (nr, nn), 1)
        valid = ((rown % nh) == (lanen % nh)) & (lanen // nh <= rown // nh)
        fnew = off_scr[:, :nn] + fn_ref[0][:, :nn]
        attend(kn_ref[0], vn_ref[0], fnew, valid)
        o_ref[0] = acc_scr[...] / l_scr[...]


def _sample_attention(page_table, q_rows, k_new, v_new, f_new, k_pages, v_pages, f_pages, nh, dh, pp):
    nb, npages = page_table.shape
    nr = q_rows.shape[1]
    rows = k_pages.shape[1]
    l2 = f_pages.shape[2]
    steps = npages // pp
    fixed = lambda shape: pl.BlockSpec((1,) + shape, lambda b, s, pt: (b, 0, 0))

    def paged(shape, i):
        return pl.BlockSpec((1,) + shape, lambda b, s, pt: (pt[b, s * pp + i], 0, 0))

    in_specs = [fixed((nr, dh)), fixed(k_new.shape[1:]), fixed(v_new.shape[1:]), fixed((1, l2))]
    in_specs += [paged((rows, dh), i) for i in range(pp)]
    in_specs += [paged((rows, dh), i) for i in range(pp)]
    in_specs += [paged((1, l2), i) for i in range(pp)]
    return pl.pallas_call(
        functools.partial(_sattn_kernel, pp=pp, nh=nh, scale=dh ** -0.5),
        grid_spec=pltpu.PrefetchScalarGridSpec(
            num_scalar_prefetch=1,
            grid=(nb, steps),
            in_specs=in_specs,
            out_specs=pl.BlockSpec((1, nr, dh), lambda b, s, pt: (b, 0, 0)),
            scratch_shapes=[pltpu.VMEM((nr, 1), F32), pltpu.VMEM((nr, 1), F32), pltpu.VMEM((nr, dh), F32),
                            pltpu.VMEM((1, l2 // 2), F32)]),
        out_shape=jax.ShapeDtypeStruct((nb, nr, dh), F32),
        compiler_params=_params(("arbitrary", "arbitrary")),
        name="sattn",
    )(page_table, q_rows, k_new, v_new, f_new, *([k_pages] * pp), *([v_pages] * pp), *([f_pages] * pp))


def _outproj_kernel(att_ref, pool_ref, x_ref, gt_ref, gpost_ref, gpre_ref, sc_ref, sh_ref, w_ref,
                    x1_ref, h2_ref):
    half = att_ref.shape[1]
    m = (jnp.dot(att_ref[...], w_ref[0:half, :], preferred_element_type=F32)
         + jnp.dot(pool_ref[...], w_ref[half:, :], preferred_element_type=F32))
    x1 = x_ref[...] + gt_ref[0] * (_rms(m) * gpost_ref[...])
    x1_ref[...] = x1
    h2 = (_rms(x1) * gpre_ref[...]) * (1.0 + sc_ref[0]) + sh_ref[0]
    h2_ref[...] = h2.astype(BF16)


def _outproj(att, pool, x, gt, g_post, g_pre, sc, sh, w_out, tm, rows_per_mod):
    m, d = x.shape
    half = att.shape[1]
    mod_spec = pl.BlockSpec((1,) + gt.shape[1:], lambda i: ((i * tm) // rows_per_mod, 0, 0))
    vec_spec = pl.BlockSpec((1, d), lambda i: (0, 0))
    return pl.pallas_call(
        _outproj_kernel,
        grid=(m // tm,),
        in_specs=[pl.BlockSpec((tm, half), lambda i: (i, 0)),
                  pl.BlockSpec((tm, half), lambda i: (i, 0)),
                  pl.BlockSpec((tm, d), lambda i: (i, 0)),
                  mod_spec, vec_spec, vec_spec, mod_spec, mod_spec,
                  pl.BlockSpec(w_out.shape, lambda i: (0, 0))],
        out_specs=[pl.BlockSpec((tm, d), lambda i: (i, 0)), pl.BlockSpec((tm, d), lambda i: (i, 0))],
        out_shape=[jax.ShapeDtypeStruct((m, d), F32), jax.ShapeDtypeStruct((m, d), BF16)],
        compiler_params=_params(("arbitrary",)),
        name="outproj",
    )(att, pool, x, gt, g_post.reshape(1, d), g_pre.reshape(1, d), sc, sh, w_out)


def _pscore_kernel(h_ref, wq_ref, sk_ref, s_ref):
    q = jnp.dot(h_ref[...], wq_ref[...], preferred_element_type=F32).astype(BF16)
    half = sk_ref.shape[2]
    for g in range(sk_ref.shape[0]):
        s_ref[g] = lax.dot_general(sk_ref[g], q[:, g * half:(g + 1) * half], _NT, preferred_element_type=F32)


def _pscore(h2, w_q, sub_keys, tm):
    m, d = h2.shape
    ng, nk, half = sub_keys.shape
    return pl.pallas_call(
        _pscore_kernel,
        grid=(m // tm,),
        in_specs=[pl.BlockSpec((tm, d), lambda i: (i, 0)),
                  pl.BlockSpec(w_q.shape, lambda i: (0, 0)),
                  pl.BlockSpec(sub_keys.shape, lambda i: (0, 0, 0))],
        out_specs=pl.BlockSpec((ng, nk, tm), lambda i: (0, 0, i)),
        out_shape=jax.ShapeDtypeStruct((ng, nk, m), F32),
        compiler_params=_params(("arbitrary",)),
        name="pscore",
    )(h2, w_q, sub_keys)


def _top_sorted(s, k):
    cols = s.shape[1]
    rank = lax.broadcasted_iota(jnp.int32, (k, cols), 0).astype(F32)

    def body(_, carry):
        rem, out, filled = carry
        cur = jnp.max(rem, axis=0, keepdims=True)
        eq = rem == cur
        cnt = jnp.sum(eq.astype(F32), axis=0, keepdims=True)
        out = jnp.where((rank >= filled) & (rank < filled + cnt), cur, out)
        return jnp.where(eq, -jnp.inf, rem), out, filled + cnt

    _, out, _ = lax.fori_loop(0, k, body, (s, jnp.zeros((k, cols), F32), jnp.zeros((1, cols), F32)))
    return out


def _kth_largest(c, k):
    cols = c.shape[1]

    def body(_, carry):
        rem, tau, filled = carry
        cur = jnp.max(rem, axis=0, keepdims=True)
        eq = rem == cur
        cnt = jnp.sum(eq.astype(F32), axis=0, keepdims=True)
        tau = jnp.where((filled < k) & (filled + cnt >= k), cur, tau)
        return jnp.where(eq, -jnp.inf, rem), tau, filled + cnt

    _, tau, _ = lax.fori_loop(0, k, body, (c, jnp.zeros((1, cols), F32), jnp.zeros((1, cols), F32)))
    return tau


def _pthresh_kernel(s_ref, e1_ref, e2_ref, tau_ref):
    k = PEER_TOPK
    s1 = s_ref[0]
    s2 = s_ref[1]
    a = _top_sorted(s1, k)
    b = _top_sorted(s2, k)
    row8 = lax.broadcasted_iota(jnp.int32, (8, a.shape[1]), 0)
    cands = [a[0:1] + b]
    for i in range(1, 8):
        cands.append(jnp.where(row8 < k // (i + 1), a[i:i + 1] + b[0:8], -jnp.inf))
    cands.append(a[8:16] + b[0:1])
    cand = jnp.concatenate(cands, axis=0)
    tau = _kth_largest(cand, k)
    top = a[0:1] + b[0:1]
    z = jnp.sum(jnp.where(cand >= tau, jnp.exp(cand - top), 0.0), axis=0, keepdims=True)
    tau_ref[0] = tau
    e1_ref[0] = jnp.exp(s1 - a[0:1]) / z
    e2_ref[0] = jnp.exp(s2 - b[0:1])


def _pthresh(s_t, tc):
    ng, nk, m = s_t.shape
    nh = ng // 2
    tc = min(tc, m)
    return pl.pallas_call(
        _pthresh_kernel,
        grid=(m // tc, nh),
        in_specs=[pl.BlockSpec((2, nk, tc), lambda i, h: (h, 0, i))],
        out_specs=[pl.BlockSpec((1, nk, tc), lambda i, h: (h, 0, i)),
                   pl.BlockSpec((1, nk, tc), lambda i, h: (h, 0, i)),
                   pl.BlockSpec((1, 1, tc), lambda i, h: (h, 0, i))],
        out_shape=[jax.ShapeDtypeStruct((nh, nk, m), F32),
                   jax.ShapeDtypeStruct((nh, nk, m), F32),
                   jax.ShapeDtypeStruct((nh, 1, m), F32)],
        compiler_params=_params(("arbitrary", "arbitrary")),
        name="pthresh",
    )(s_t)


def _pexpert_kernel(h_ref, s_ref, e1_ref, e2_ref, tau_ref, u_ref, v_ref, f_ref, *, isub):
    sl = pl.program_id(1)
    nk = s_ref.shape[1]
    nh = e1_ref.shape[0]

    @pl.when(sl == 0)
    def _():
        f_ref[...] = jnp.zeros(f_ref.shape, F32)

    act = lax.dot_general(u_ref[...], h_ref[...], _NT, preferred_element_type=F32)
    gates = []
    for ii in range(isub):
        i0 = sl * isub + ii
        w = jnp.zeros((nk, h_ref.shape[0]), F32)
        for h in range(nh):
            s1 = s_ref[2 * h, pl.ds(i0, 1), :]
            e1 = e1_ref[h, pl.ds(i0, 1), :]
            w = w + jnp.where(s_ref[2 * h + 1] + s1 >= tau_ref[h], e2_ref[h] * e1, 0.0)
        gates.append(w)
    gate = jnp.concatenate(gates, axis=0)
    gelu = 0.5 * act * (1.0 + lax.erf(act * (2.0 ** -0.5)))
    wts = (gate * gelu).astype(BF16)
    f_ref[...] += lax.dot_general(wts, v_ref[...], _TN, preferred_element_type=F32)


def _pexpert(h2, s_t, e1, e2, tau, u_tab, v_tab, tm, isub):
    m, d = h2.shape
    ng, nk, _ = s_t.shape
    nh = ng // 2
    ne = u_tab.shape[0]
    te = isub * nk
    return pl.pallas_call(
        functools.partial(_pexpert_kernel, isub=isub),
        grid=(m // tm, ne // te),
        in_specs=[pl.BlockSpec((tm, d), lambda i, s: (i, 0)),
                  pl.BlockSpec((ng, nk, tm), lambda i, s: (0, 0, i)),
                  pl.BlockSpec((nh, nk, tm), lambda i, s: (0, 0, i)),
                  pl.BlockSpec((nh, nk, tm), lambda i, s: (0, 0, i)),
                  pl.BlockSpec((nh, 1, tm), lambda i, s: (0, 0, i)),
                  pl.BlockSpec((te, d), lambda i, s: (s, 0)),
                  pl.BlockSpec((te, d), lambda i, s: (s, 0))],
        out_specs=pl.BlockSpec((tm, d), lambda i, s: (i, 0)),
        out_shape=jax.ShapeDtypeStruct((m, d), F32),
        compiler_params=_params(("arbitrary", "arbitrary")),
        name="pexpert",
    )(h2, s_t, e1, e2, tau, u_tab, v_tab)


def _final_kernel(x_ref, f_ref, gt_ref, g_ref, o_ref):
    o_ref[...] = x_ref[...] + gt_ref[0] * (_rms(f_ref[...]) * g_ref[...])


def _final(x1, f, gt, g_post, tm, rows_per_mod):
    m, d = x1.shape
    return pl.pallas_call(
        _final_kernel,
        grid=(m // tm,),
        in_specs=[pl.BlockSpec((tm, d), lambda i: (i, 0)),
                  pl.BlockSpec((tm, d), lambda i: (i, 0)),
                  pl.BlockSpec((1,) + gt.shape[1:], lambda i: ((i * tm) // rows_per_mod, 0, 0)),
                  pl.BlockSpec((1, d), lambda i: (0, 0))],
        out_specs=pl.BlockSpec((tm, d), lambda i: (i, 0)),
        out_shape=jax.ShapeDtypeStruct((m, d), F32),
        compiler_params=_params(("arbitrary",)),
        name="final",
    )(x1, f, gt, g_post.reshape(1, d))


def _tile(m, want):
    t = min(want, m)
    while m % t:
        t //= 2
    return t


def _ffn(x1, h2, gt2, g_post_ffn, w_q, sub_keys, u_tab, v_tab, tm, rows_per_mod):
    m = x1.shape[0]
    s_t = _pscore(h2, w_q, sub_keys, _tile(m, 512))
    e1, e2, tau = _pthresh(s_t, 256)
    f = _pexpert(h2, s_t, e1, e2, tau, u_tab, v_tab, _tile(m, 512), isub=4)
    return _final(x1, f, gt2, g_post_ffn, tm, rows_per_mod)


def kernel(x_prompt, x_sample, c_prompt, c_sample, cache_k, cache_v, cache_logf, state_pool, page_table, w_ada, b_ada, g_pre_mix, g_post_mix, g_pre_ffn, g_post_ffn, w_in, b_f, w_pool, pool_scale, w_out, w_peer_q, peer_subkeys, peer_u, peer_v):
    depth = w_ada.shape[0]
    assert depth == 1, "single-layer step"
    nbp, t, d = x_prompt.shape
    nbs, ts, _ = x_sample.shape
    _, n_pool, page, nh, dh = cache_k.shape
    assert page == PAGE_SIZE
    att_w = nh * dh
    pool_w = state_pool.shape[-1]
    npages = page_table.shape[1]
    past = npages * page
    l = 0

    w_main = jnp.concatenate([w_in[l, :, :3 * att_w], w_in[l, :, 3 * att_w + nh:]], axis=1).astype(BF16)
    w_fg_t = jnp.zeros((16, d), BF16).at[:nh].set(w_in[l, :, 3 * att_w:3 * att_w + nh].T.astype(BF16))
    w_out_b = w_out[l].astype(BF16)
    w_q_b = w_peer_q[l].astype(BF16)
    ng = peer_subkeys.shape[1] * peer_subkeys.shape[2]
    sub_keys = peer_subkeys[l].reshape(ng, peer_subkeys.shape[3], peer_subkeys.shape[4]).astype(BF16)
    u_b = peer_u[l].astype(BF16)
    v_b = peer_v[l].astype(BF16)

    nc = nbp + nbs
    c_all = jnp.zeros((-(-nc // 8) * 8, d), F32).at[:nc].set(jnp.concatenate([c_prompt, c_sample], axis=0))
    ada = _ada(c_all, w_ada[l], b_ada[l])
    mods_p = [ada[:nbp, i * d:(i + 1) * d].reshape(nbp, 1, d) for i in range(6)]
    ms = nbs * ts
    mods_s = [jnp.repeat(ada[nbp:nc, i * d:(i + 1) * d], ts, axis=0).reshape(1, ms, d) for i in range(6)]

    mp = nbp * t
    tmp = _tile(t, 256)
    xp = x_prompt.reshape(mp, d)
    q, k, v, p, lf_t = _inproj(xp, mods_p[1], mods_p[0], g_pre_mix[l], w_main, w_fg_t, b_f[l], tmp, t)
    f_t = _cumsum(lf_t, nbp)
    ta = _tile(t, 512)
    att = _prompt_attention(q.reshape(nbp, t, att_w), k.reshape(nbp, t, att_w), v.reshape(nbp, t, att_w),
                            f_t.reshape(nh, 1, mp), nbp, nh, dh, ta, ta)
    p3 = p.reshape(nbp, t, pool_w)
    pool = _pool_prompt(p3, w_pool[l], pool_scale[l])
    x1, h2 = _outproj(att.reshape(mp, att_w), pool.reshape(mp, pool_w), xp, mods_p[2], g_post_mix[l],
                      g_pre_ffn[l], mods_p[4], mods_p[3], w_out_b, tmp, t)
    yp = _ffn(x1, h2, mods_p[5], g_post_ffn[l], w_q_b, sub_keys, u_b, v_b, tmp, t).reshape(nbp, t, d)
    k_prompt = k.reshape(1, nbp, t, nh, dh)
    v_prompt = v.reshape(1, nbp, t, nh, dh)
    logf_prompt = lf_t.T.reshape(1, nbp, t, nh)
    pool_prompt = p3[None, :, t - POOL_HIST:, :]

    xs = x_sample.reshape(ms, d)
    qs, ks, vs, ps, lfs_t = _inproj(xs, mods_s[1], mods_s[0], g_pre_mix[l], w_main, w_fg_t, b_f[l], ms, ms)
    lanes = page * nh
    lf_pages = cache_logf[l].reshape(n_pool, lanes)
    pf_pages = _lfscan(lf_pages, nh, 256).reshape(n_pool, 1, 2 * lanes)
    lf_new = jnp.zeros((nbs, lanes), F32).at[:, :ts * nh].set(lfs_t.T.reshape(nbs, ts * nh))
    pf_new = _lfscan(lf_new, nh, 256).reshape(nbs, 1, 2 * lanes)
    att_s = _sample_attention(page_table, qs.reshape(nbs, ts * nh, dh), ks.reshape(nbs, ts * nh, dh),
                              vs.reshape(nbs, ts * nh, dh), pf_new,
                              cache_k[l].reshape(n_pool, lanes, dh), cache_v[l].reshape(n_pool, lanes, dh),
                              pf_pages, nh, dh, pp=4 if npages % 4 == 0 else 1)
    pe = jnp.concatenate([state_pool[l], ps.reshape(nbs, ts, pool_w)], axis=1)
    pool_s = _pool_sample(jnp.swapaxes(pe, 0, 1), w_pool[l], pool_scale[l], past)
    pool_s = jnp.swapaxes(pool_s, 0, 1).reshape(ms, pool_w)
    x1s, h2s = _outproj(att_s.reshape(ms, att_w).astype(BF16), pool_s, xs, mods_s[2], g_post_mix[l],
                        g_pre_ffn[l], mods_s[4], mods_s[3], w_out_b, ms, ms)
    ys = _ffn(x1s, h2s, mods_s[5], g_post_ffn[l], w_q_b, sub_keys, u_b, v_b, ms, ms).reshape(nbs, ts, d)
    k_sample = ks.reshape(1, nbs, ts, nh, dh)
    v_sample = vs.reshape(1, nbs, ts, nh, dh)
    logf_sample = lfs_t.T.reshape(1, nbs, ts, nh)
    pool_sample = pe[None, :, ts:, :]

    return (yp, ys, k_prompt, v_prompt, logf_prompt, pool_prompt, k_sample, v_sample, logf_sample, pool_sample)
```
